```python
import jax, jax.numpy as jnp
from jax import lax
import numpy as np

D_MODEL = 1024
BATCH = 8
SEQ = 4096
DEPTH = 2
DEC_BATCH = 32
DEC_SEQ = 4
PAST_LEN = 16384
PAGE_SIZE = 128

HEAD_DIM = 64
N_HEADS = 12
N_KV_HEADS = 4
GROUP = N_HEADS // N_KV_HEADS
MEM_HEADS = 4
MEM_LEN = 256
D_MIX = (N_HEADS + MEM_HEADS) * HEAD_DIM
IDX_HEADS = 8
IDX_DIM = 64
DSA_TOPK = 256
MOBA_BLOCK = 256
MOBA_TOPK = 3
ROPE_THETA = 10000.0
RMS_EPS = 1e-6
DSA_QBLOCK = 128
MOBA_QBLOCK = 16
N_MIXERS = 2
N_A_LAYERS = (DEPTH + 1) // 2
N_B_LAYERS = DEPTH // 2
Q_W = N_HEADS * HEAD_DIM
KV_W = N_KV_HEADS * HEAD_DIM
MQ_W = MEM_HEADS * HEAD_DIM
COLS_B = Q_W + 2 * KV_W + MQ_W + D_MIX
COLS_A = COLS_B + IDX_HEADS * IDX_DIM + IDX_DIM + IDX_HEADS

kernel_name = "dsa_moba_interleaved_gated_memory_decoder_step"


def rms_norm(x, g):
    xf = x.astype(jnp.float32)
    y = xf * lax.rsqrt(jnp.mean(xf * xf, axis=-1, keepdims=True) + RMS_EPS)
    return (y * g.astype(jnp.float32)).astype(x.dtype)


def rope(x, pos):
    half = x.shape[-1] // 2
    inv = ROPE_THETA ** (-jnp.arange(half, dtype=jnp.float32) / half)
    ang = pos.astype(jnp.float32)[:, None] * inv[None, :]
    cos = jnp.cos(ang)[None, :, None, :]
    sin = jnp.sin(ang)[None, :, None, :]
    xf = x.astype(jnp.float32)
    x1, x2 = xf[..., :half], xf[..., half:]
    return jnp.concatenate([x1 * cos - x2 * sin, x2 * cos + x1 * sin], axis=-1).astype(x.dtype)


def paged_rows(pool, page_table, new_rows, pos, head=None):
    B, n_pages = page_table.shape
    psz = pool.shape[1]
    past = n_pages * psz
    bidx = jnp.arange(B).reshape((B,) + (1,) * (pos.ndim - 1))
    pp = jnp.clip(pos, 0, past - 1)
    phys = page_table[bidx, pp // psz]
    nn = jnp.clip(pos - past, 0, new_rows.shape[1] - 1)
    if head is None:
        old = pool[phys, pp % psz]
        new = new_rows[bidx, nn]
    else:
        old = pool[phys, pp % psz, head]
        new = new_rows[bidx, nn, head]
    is_new = (pos >= past).reshape(pos.shape + (1,) * (old.ndim - pos.ndim))
    return jnp.where(is_new, new, old)


def project(x, pos, norm_g, w_in, q_g, k_g, mq_g, idx_k_g):
    B, T, _ = x.shape
    widths = [Q_W, KV_W, KV_W, MQ_W, D_MIX]
    if idx_k_g is not None:
        widths = widths + [IDX_HEADS * IDX_DIM, IDX_DIM, IDX_HEADS]
    parts = jnp.split(rms_norm(x, norm_g) @ w_in, np.cumsum(widths)[:-1].tolist(), axis=-1)
    q = rope(rms_norm(parts[0].reshape(B, T, N_HEADS, HEAD_DIM), q_g), pos)
    k = rope(rms_norm(parts[1].reshape(B, T, N_KV_HEADS, HEAD_DIM), k_g), pos)
    v = parts[2].reshape(B, T, N_KV_HEADS, HEAD_DIM)
    qm = rms_norm(parts[3].reshape(B, T, MEM_HEADS, HEAD_DIM), mq_g)
    gate = parts[4]
    if idx_k_g is None:
        return (q, k, v, qm, gate), None
    qi = rope(parts[5].reshape(B, T, IDX_HEADS, IDX_DIM), pos)
    ki = rope(rms_norm(parts[6], idx_k_g)[:, :, None, :], pos)[:, :, 0, :]
    wi = parts[7] * IDX_HEADS ** -0.5
    return (q, k, v, qm, gate), (qi, ki, wi)


def dsa_core(q, qi, wi, q_pos, kidx, gather_kv, n_sel):
    B, Tq = q.shape[:2]
    L = kidx.shape[1]
    s = jnp.einsum("bqhd,bld->bqhl", qi, kidx) * IDX_DIM ** -0.5
    score = jnp.einsum("bqhl,bqh->bql", jax.nn.relu(s), wi).astype(jnp.float32)
    causal = jnp.arange(L)[None, :] <= q_pos[:, None]
    score = jnp.where(causal[None], score, -jnp.inf)
    _, idx = lax.top_k(score, n_sel)
    valid = idx <= q_pos[None, :, None]
    k_sel, v_sel = gather_kv(idx)
    qg = q.reshape(B, Tq, N_KV_HEADS, GROUP, HEAD_DIM)
    logits = jnp.einsum("bqkgd,bqnkd->bqkgn", qg, k_sel) * HEAD_DIM ** -0.5
    logits = jnp.where(valid[:, :, None, None, :], logits, -jnp.inf)
    p = jax.nn.softmax(logits.astype(jnp.float32), axis=-1).astype(v_sel.dtype)
    out = jnp.einsum("bqkgn,bqnkd->bqkgd", p, v_sel)
    return out.reshape(B, Tq, Q_W)


def dsa_prompt(q, k, v, qi, ki, wi):
    B, S = q.shape[:2]
    n_sel = min(DSA_TOPK, S // 4)
    nblk = S // DSA_QBLOCK
    take = jax.vmap(lambda a, i: a[i])

    def blockify(a):
        return jnp.moveaxis(a.reshape((B, nblk, DSA_QBLOCK) + a.shape[2:]), 1, 0)

    def gather_kv(idx):
        return take(k, idx), take(v, idx)

    pos_blocks = jnp.arange(S, dtype=jnp.int32).reshape(nblk, DSA_QBLOCK)
    out = lax.map(lambda a: dsa_core(a[0], a[1], a[2], a[3], ki, gather_kv, n_sel),
                  (blockify(q), blockify(qi), blockify(wi), pos_blocks))
    return jnp.moveaxis(out, 0, 1).reshape(B, S, Q_W)


def dsa_sample(q, k_new, v_new, qi, ki_new, wi, pool_k, pool_v, pool_ki, page_table, pos):
    B, T = q.shape[:2]
    past = page_table.shape[1] * pool_ki.shape[1]
    ki_past = pool_ki[page_table].reshape(B, past, IDX_DIM)
    ki_all = jnp.concatenate([ki_past, ki_new], axis=1)
    n_sel = min(DSA_TOPK, (past + T) // 4)

    def gather_kv(idx):
        return (paged_rows(pool_k, page_table, k_new, idx),
                paged_rows(pool_v, page_table, v_new, idx))

    return dsa_core(q, qi, wi, pos, ki_all, gather_kv, n_sel)


def block_means(k, n):
    B = k.shape[0]
    kb = k[:, : n * MOBA_BLOCK].reshape(B, n, MOBA_BLOCK, N_KV_HEADS, HEAD_DIM)
    return jnp.mean(kb.astype(jnp.float32), axis=2).astype(k.dtype)


def moba_core(q, q_pos, k_means, gather_blocks, gather_own, n_sel):
    B, Tq = q.shape[:2]
    scale = HEAD_DIM ** -0.5
    own_blk = q_pos // MOBA_BLOCK
    own_pos = own_blk[:, None] * MOBA_BLOCK + jnp.arange(MOBA_BLOCK)
    k_own, v_own = gather_own(own_pos)
    qg = q.reshape(B, Tq, N_KV_HEADS, GROUP, HEAD_DIM)
    own_logits = (jnp.einsum("bqkgd,bqnkd->bqkgn", qg, k_own) * scale).reshape(B, Tq, N_HEADS, MOBA_BLOCK)
    own_logits = jnp.where((own_pos <= q_pos[:, None])[None, :, None, :], own_logits, -jnp.inf)
    if n_sel == 0:
        p = jax.nn.softmax(own_logits.astype(jnp.float32), axis=-1).astype(v_own.dtype)
        out = jnp.einsum("bqkgn,bqnkd->bqkgd", p.reshape(B, Tq, N_KV_HEADS, GROUP, MOBA_BLOCK), v_own)
        return out.reshape(B, Tq, Q_W)
    n_cand = k_means.shape[1]
    blk_score = jnp.einsum("bqkgd,bjkd->bqkgj", qg, k_means).reshape(B, Tq, N_HEADS, n_cand)
    past_ok = jnp.arange(n_cand)[None, :] < own_blk[:, None]
    blk_score = jnp.where(past_ok[None, :, None, :], blk_score.astype(jnp.float32), -jnp.inf)
    _, blk = lax.top_k(blk_score, n_sel)
    blk_ok = blk < own_blk[None, :, None, None]
    k_sel, v_sel = gather_blocks(blk)
    sel_logits = jnp.einsum("bqhd,bqhsnd->bqhsn", q, k_sel) * scale
    sel_logits = jnp.where(blk_ok[..., None], sel_logits, -jnp.inf).reshape(B, Tq, N_HEADS, n_sel * MOBA_BLOCK)
    p = jax.nn.softmax(jnp.concatenate([sel_logits, own_logits], axis=-1).astype(jnp.float32), axis=-1)
    p_sel = p[..., : n_sel * MOBA_BLOCK].reshape(B, Tq, N_HEADS, n_sel, MOBA_BLOCK).astype(v_sel.dtype)
    p_own = p[..., n_sel * MOBA_BLOCK:].reshape(B, Tq, N_KV_HEADS, GROUP, MOBA_BLOCK).astype(v_own.dtype)
    out = (jnp.einsum("bqhsn,bqhsnd->bqhd", p_sel, v_sel)
           + jnp.einsum("bqkgn,bqnkd->bqkgd", p_own, v_own).reshape(B, Tq, N_HEADS, HEAD_DIM))
    return out.reshape(B, Tq, Q_W)


def moba_prompt(q, k, v):
    B, S = q.shape[:2]
    n_blocks = -(-S // MOBA_BLOCK)
    pad = n_blocks * MOBA_BLOCK - S
    n_full = (S - 1) // MOBA_BLOCK
    n_sel = min(MOBA_TOPK, n_full)
    k_means = block_means(k, n_full) if n_sel > 0 else None
    kp = jnp.pad(k, ((0, 0), (0, pad), (0, 0), (0, 0)))
    vp = jnp.pad(v, ((0, 0), (0, pad), (0, 0), (0, 0)))
    kb = kp.reshape(B, n_blocks, MOBA_BLOCK, N_KV_HEADS, HEAD_DIM).transpose(0, 1, 3, 2, 4)
    vb = vp.reshape(B, n_blocks, MOBA_BLOCK, N_KV_HEADS, HEAD_DIM).transpose(0, 1, 3, 2, 4)
    bidx = jnp.arange(B)[:, None, None, None]
    hk = (jnp.arange(N_HEADS) // GROUP)[None, None, :, None]

    def gather_blocks(blk):
        return kb[bidx, blk, hk], vb[bidx, blk, hk]

    def gather_own(p):
        return kp[:, p], vp[:, p]

    nq = S // MOBA_QBLOCK
    qc = jnp.moveaxis(q.reshape(B, nq, MOBA_QBLOCK, N_HEADS, HEAD_DIM), 1, 0)
    pc = jnp.arange(S, dtype=jnp.int32).reshape(nq, MOBA_QBLOCK)
    out = lax.map(lambda a: moba_core(a[0], a[1], k_means, gather_blocks, gather_own, n_sel), (qc, pc))
    return jnp.moveaxis(out, 0, 1).reshape(B, S, Q_W)


def moba_sample(q, k_new, v_new, pool_k, pool_v, page_table, pos):
    B, T = q.shape[:2]
    past = page_table.shape[1] * pool_k.shape[1]
    n_full = (past + T - 1) // MOBA_BLOCK
    n_sel = min(MOBA_TOPK, n_full)
    k_means = None
    if n_sel > 0:
        k_past = pool_k[page_table].reshape(B, past, N_KV_HEADS, HEAD_DIM)
        k_means = block_means(jnp.concatenate([k_past, k_new], axis=1), n_full)
    hk = (jnp.arange(N_HEADS) // GROUP)[None, None, :, None, None]

    def gather_blocks(blk):
        p = blk[..., None] * MOBA_BLOCK + jnp.arange(MOBA_BLOCK)
        return (paged_rows(pool_k, page_table, k_new, p, hk),
                paged_rows(pool_v, page_table, v_new, p, hk))

    def gather_own(p):
        pb = jnp.broadcast_to(p[None], (B,) + p.shape)
        return (paged_rows(pool_k, page_table, k_new, pb),
                paged_rows(pool_v, page_table, v_new, pb))

    return moba_core(q, pos, k_means, gather_blocks, gather_own, n_sel)


def memory_kv(mem, mem_norm_g, w_mem_kv, mk_g):
    B, M, _ = mem.shape
    mk, mv = jnp.split(rms_norm(mem, mem_norm_g) @ w_mem_kv, 2, axis=-1)
    mk = rms_norm(mk.reshape(B, M, MEM_HEADS, HEAD_DIM), mk_g)
    return mk, mv.reshape(B, M, MEM_HEADS, HEAD_DIM)


def memory_attend(qm, mk, mv):
    B, T = qm.shape[:2]
    logits = jnp.einsum("bthd,bmhd->bhtm", qm, mk) * HEAD_DIM ** -0.5
    p = jax.nn.softmax(logits.astype(jnp.float32), axis=-1).astype(mv.dtype)
    return jnp.einsum("bhtm,bmhd->bthd", p, mv).reshape(B, T, MQ_W)


def finish(x, mix_out, mem_out, gate, w_out):
    return x + (jnp.concatenate([mix_out, mem_out], axis=-1) * jax.nn.silu(gate)) @ w_out


def setup_inputs(seed: int = 0) -> dict:
    key = jax.random.key(seed)
    ks = jax.random.split(key, 24)
    n_pages = PAST_LEN // PAGE_SIZE
    n_used = DEC_BATCH * n_pages
    n_pool = n_used + max(1, n_used // 4)
    f32 = jnp.float32

    def nrm(k, shape, scale=1.0):
        return jax.random.normal(k, shape, f32) * scale

    def gain(k, shape):
        return 1.0 + 0.02 * jax.random.normal(k, shape, f32)

    page_table = jax.random.permutation(ks[7], n_pool)[:n_used].reshape(DEC_BATCH, n_pages).astype(jnp.int32)
    return {
        "x_prompt": nrm(ks[0], (BATCH, SEQ, D_MODEL)),
        "x_sample": nrm(ks[1], (DEC_BATCH, DEC_SEQ, D_MODEL)),
        "cache_k": nrm(ks[2], (DEPTH, n_pool, PAGE_SIZE, N_KV_HEADS, HEAD_DIM)),
        "cache_v": nrm(ks[3], (DEPTH, n_pool, PAGE_SIZE, N_KV_HEADS, HEAD_DIM)),
        "cache_kidx": nrm(ks[4], (N_A_LAYERS, n_pool, PAGE_SIZE, IDX_DIM)),
        "cache_mem_k": nrm(ks[5], (DEPTH, DEC_BATCH, MEM_LEN, MEM_HEADS, HEAD_DIM)),
        "cache_mem_v": nrm(ks[6], (DEPTH, DEC_BATCH, MEM_LEN, MEM_HEADS, HEAD_DIM)),
        "page_table": page_table,
        "mem_prompt": nrm(ks[8], (BATCH, MEM_LEN, D_MODEL)),
        "norm_g": gain(ks[9], (DEPTH, D_MODEL)),
        "w_in_a": nrm(ks[10], (N_A_LAYERS, D_MODEL, COLS_A), D_MODEL ** -0.5),
        "w_in_b": nrm(ks[11], (N_B_LAYERS, D_MODEL, COLS_B), D_MODEL ** -0.5),
        "q_norm_g": gain(ks[12], (DEPTH, HEAD_DIM)),
        "k_norm_g": gain(ks[13], (DEPTH, HEAD_DIM)),
        "idx_k_norm_g": gain(ks[14], (N_A_LAYERS, IDX_DIM)),
        "mem_norm_g": gain(ks[15], (DEPTH, D_MODEL)),
        "w_mem_kv": nrm(ks[16], (DEPTH, D_MODEL, 2 * MQ_W), D_MODEL ** -0.5),
        "mem_q_norm_g": gain(ks[17], (DEPTH, HEAD_DIM)),
        "mem_k_norm_g": gain(ks[18], (DEPTH, HEAD_DIM)),
        "w_out": nrm(ks[19], (DEPTH, D_MIX, D_MODEL), D_MIX ** -0.5),
    }


def reference(x_prompt, x_sample, cache_k, cache_v, cache_kidx, cache_mem_k, cache_mem_v, page_table,
              mem_prompt, norm_g, w_in_a, w_in_b, q_norm_g, k_norm_g, idx_k_norm_g, mem_norm_g,
              w_mem_kv, mem_q_norm_g, mem_k_norm_g, w_out):
    S = x_prompt.shape[1]
    T = x_sample.shape[1]
    past = page_table.shape[1] * cache_k.shape[2]
    pos_p = jnp.arange(S, dtype=jnp.int32)
    pos_s = past + jnp.arange(T, dtype=jnp.int32)
    xp, xs = x_prompt, x_sample
    k_p, v_p, ki_p, mk_p, mv_p, k_s, v_s, ki_s = [], [], [], [], [], [], [], []
    for i in range(DEPTH):
        j = i // N_MIXERS
        use_dsa = (i % N_MIXERS) == 0
        w_in = w_in_a[j] if use_dsa else w_in_b[j]
        idx_g = idx_k_norm_g[j] if use_dsa else None
        (qp, kp_, vp_, qmp, gp), idxp = project(xp, pos_p, norm_g[i], w_in, q_norm_g[i], k_norm_g[i], mem_q_norm_g[i], idx_g)
        (qs, ks_, vs_, qms, gs), idxs = project(xs, pos_s, norm_g[i], w_in, q_norm_g[i], k_norm_g[i], mem_q_norm_g[i], idx_g)
        if use_dsa:
            mix_p = dsa_prompt(qp, kp_, vp_, idxp[0], idxp[1], idxp[2])
            mix_s = dsa_sample(qs, ks_, vs_, idxs[0], idxs[1], idxs[2], cache_k[i], cache_v[i],
                               cache_kidx[j], page_table, pos_s)
            ki_p.append(idxp[1])
            ki_s.append(idxs[1])
        else:
            mix_p = moba_prompt(qp, kp_, vp_)
            mix_s = moba_sample(qs, ks_, vs_, cache_k[i], cache_v[i], page_table, pos_s)
        mkp, mvp = memory_kv(mem_prompt, mem_norm_g[i], w_mem_kv[i], mem_k_norm_g[i])
        xp = finish(xp, mix_p, memory_attend(qmp, mkp, mvp), gp, w_out[i])
        xs = finish(xs, mix_s, memory_attend(qms, cache_mem_k[i], cache_mem_v[i]), gs, w_out[i])
        k_p.append(kp_)
        v_p.append(vp_)
        mk_p.append(mkp)
        mv_p.append(mvp)
        k_s.append(ks_)
        v_s.append(vs_)
    return (xp, xs, jnp.stack(k_p), jnp.stack(v_p), jnp.stack(ki_p), jnp.stack(mk_p), jnp.stack(mv_p),
            jnp.stack(k_s), jnp.stack(v_s), jnp.stack(ki_s))
```

```python
import functools

import numpy as np
import jax
import jax.numpy as jnp
from jax import lax
from jax.experimental import pallas as pl
from jax.experimental.pallas import tpu as pltpu

HEAD_DIM = 64
N_HEADS = 12
N_KV_HEADS = 4
GROUP = N_HEADS // N_KV_HEADS
MEM_HEADS = 4
IDX_HEADS = 8
IDX_DIM = 64
DSA_TOPK = 256
MOBA_BLOCK = 256
MOBA_TOPK = 3
ROPE_THETA = 10000.0
RMS_EPS = 1e-6
Q_W = N_HEADS * HEAD_DIM
KV_W = N_KV_HEADS * HEAD_DIM
MQ_W = MEM_HEADS * HEAD_DIM
D_MIX = Q_W + MQ_W
IDX_W = IDX_HEADS * IDX_DIM
COLS_B = Q_W + 2 * KV_W + MQ_W + D_MIX

LANES = 128
SUBLANES = 8
VMEM_LIMIT_BYTES = 56 * 1024 * 1024

NEG_BIG = -1e30
INT32_MIN = -(2 ** 31)
F32 = jnp.float32
BF16 = jnp.bfloat16
_NT = (((1,), (1,)), ((), ()))


def _dot_nt(a, b):
    return lax.dot_general(a, b, _NT, preferred_element_type=F32)


def _dot(a, b):
    return jnp.dot(a, b, preferred_element_type=F32)


def _params(*sem):
    return pltpu.CompilerParams(dimension_semantics=sem, vmem_limit_bytes=VMEM_LIMIT_BYTES)


def _iota(shape, dim):
    return lax.broadcasted_iota(jnp.int32, shape, dim)


def _head_group_matrix():
    r = _iota((LANES, LANES), 0) // HEAD_DIM
    c = _iota((LANES, LANES), 1) // HEAD_DIM
    return jnp.where(r == c, 1.0, 0.0).astype(BF16)


def _head_rms(yc, gain, hm):
    ss = yc * yc
    hi = ss.astype(BF16)
    lo = (ss - hi.astype(F32)).astype(BF16)
    msq = (_dot(hi, hm) + _dot(lo, hm)) * (1.0 / HEAD_DIM)
    return yc * lax.rsqrt(msq + RMS_EPS) * gain


def _rope(yc, cs, sn, first_half):
    half = HEAD_DIM // 2
    rot = jnp.where(first_half, pltpu.roll(yc, LANES - half, 1), pltpu.roll(yc, half, 1))
    return yc * cs + rot * sn


def _flash_update(s, vg, m_ref, l_ref, acc_ref, g, v_transposed=False):
    tk = s.shape[1]
    m_prev = m_ref[g]
    l_prev = l_ref[g]
    m_cur = jnp.max(s, axis=1, keepdims=True)
    m_next = jnp.maximum(m_prev, m_cur)
    p = jnp.exp(s - jnp.tile(m_next, (1, tk // LANES)))
    alpha = jnp.exp(m_prev - m_next)
    l_ref[g] = alpha * l_prev + jnp.sum(p, axis=1, keepdims=True)
    m_ref[g] = m_next
    aw = acc_ref.shape[-1]
    alpha_w = alpha[:, :aw] if aw <= LANES else jnp.tile(alpha, (1, aw // LANES))
    pv = _dot_nt(p.astype(BF16), vg) if v_transposed else _dot(p.astype(BF16), vg)
    acc_ref[g] = acc_ref[g] * alpha_w + pv


def _flash_init(m_ref, l_ref, acc_ref):
    m_ref[...] = jnp.full(m_ref.shape, NEG_BIG, F32)
    l_ref[...] = jnp.zeros(l_ref.shape, F32)
    acc_ref[...] = jnp.zeros(acc_ref.shape, F32)


def _flash_finish(l_ref, acc_ref, rows):
    outs = []
    for h in range(N_HEADS):
        g, r = divmod(h, GROUP)
        sl = slice(r * rows, (r + 1) * rows)
        outs.append(acc_ref[g, sl, :] / l_ref[g, sl, :HEAD_DIM])
    return jnp.concatenate(outs, axis=1)


def _sortable(score):
    score = jnp.where(score == 0.0, 0.0, score)
    bits = pltpu.bitcast(score, jnp.int32)
    return bits ^ ((bits >> 31) & jnp.int32(0x7FFFFFFF))


def _proj_body(*refs, has_idx, has_means, tm):
    it = iter(refs)
    x_ref, ng_ref, w_ref, cs_ref, sn_ref, gq_ref, gk_ref, gm_ref = (next(it) for _ in range(8))
    gi_ref = next(it) if has_idx else None
    q_ref, k_ref, v_ref, qm_ref, gate_ref = (next(it) for _ in range(5))
    qi_ref, kiw_ref = (next(it), next(it)) if has_idx else (None, None)
    km_ref = next(it) if has_means else None

    x = x_ref[0]
    ms = jnp.mean(x * x, axis=-1, keepdims=True)
    h = (x * lax.rsqrt(ms + RMS_EPS) * ng_ref[...]).astype(BF16)
    hm = _head_group_matrix()
    cs = cs_ref[...]
    sn = sn_ref[...]
    first_half = (_iota((tm, LANES), 1) % HEAD_DIM) < (HEAD_DIM // 2)

    off = 0
    y = _dot(h, w_ref[:, off:off + Q_W])
    for c in range(Q_W // LANES):
        yc = _head_rms(y[:, c * LANES:(c + 1) * LANES], gq_ref[...], hm)
        q_ref[0, :, c * LANES:(c + 1) * LANES] = _rope(yc, cs, sn, first_half).astype(q_ref.dtype)
    off += Q_W

    y = _dot(h, w_ref[:, off:off + KV_W])
    kcs = []
    for c in range(KV_W // LANES):
        yc = _head_rms(y[:, c * LANES:(c + 1) * LANES], gk_ref[...], hm)
        kc = _rope(yc, cs, sn, first_half)
        k_ref[0, :, c * LANES:(c + 1) * LANES] = kc
        kcs.append(kc)
    off += KV_W
    if has_means:
        kk = jnp.concatenate(kcs, axis=1)
        nblk = tm // MOBA_BLOCK
        means = [jnp.mean(kk[j * MOBA_BLOCK:(j + 1) * MOBA_BLOCK], axis=0, keepdims=True)
                 for j in range(nblk)]
        km_ref[0, 0] = jnp.concatenate(means, axis=0)

    v_ref[0] = _dot(h, w_ref[:, off:off + KV_W])
    off += KV_W

    y = _dot(h, w_ref[:, off:off + MQ_W])
    for c in range(MQ_W // LANES):
        yc = _head_rms(y[:, c * LANES:(c + 1) * LANES], gm_ref[...], hm)
        qm_ref[0, :, c * LANES:(c + 1) * LANES] = yc.astype(qm_ref.dtype)
    off += MQ_W

    gate_ref[0] = _dot(h, w_ref[:, off:off + D_MIX]).astype(gate_ref.dtype)
    off += D_MIX

    if has_idx:
        y = _dot(h, w_ref[:, off:off + IDX_W])
        for c in range(IDX_W // LANES):
            yc = y[:, c * LANES:(c + 1) * LANES]
            qi_ref[0, :, c * LANES:(c + 1) * LANES] = _rope(yc, cs, sn, first_half).astype(qi_ref.dtype)
        off += IDX_W
        yc = _dot(h, w_ref[:, off:off + LANES])
        ki = _rope(_head_rms(yc, gi_ref[...], hm), cs, sn, first_half)
        lane = _iota((tm, LANES), 1)
        kiw_ref[0] = jnp.where(lane < IDX_DIM, ki, yc * (IDX_HEADS ** -0.5))


def _rope_tables(pos):
    half = HEAD_DIM // 2
    inv = ROPE_THETA ** (-jnp.arange(half, dtype=F32) / half)
    ang = pos.astype(F32)[:, None] * inv[None, :]
    cos, sin = jnp.cos(ang), jnp.sin(ang)
    cs = jnp.tile(cos, (1, LANES // half))
    sn = jnp.tile(jnp.concatenate([-sin, sin], axis=1), (1, LANES // HEAD_DIM))
    return cs, sn


def _tile_gain(g):
    return jnp.tile(g.astype(F32), LANES // HEAD_DIM)[None, :]


def _project(x, pos, norm_g, w_in, q_g, k_g, mq_g, idx_g, *, tm, has_means):
    bx, tx, d = x.shape
    has_idx = idx_g is not None
    cols = w_in.shape[1]
    cpad = -(-cols // LANES) * LANES
    w = jnp.pad(w_in, ((0, 0), (0, cpad - cols))).astype(BF16)
    cs, sn = _rope_tables(pos)
    nt = tx // tm
    assert tx % tm == 0 and (not has_means or tm % MOBA_BLOCK == 0)

    row = lambda c, dt: (pl.BlockSpec((1, tm, c), lambda b, j: (b, j, 0)),
                         jax.ShapeDtypeStruct((bx, tx, c), dt))
    const2 = lambda a: pl.BlockSpec(a.shape, lambda b, j: (0, 0))
    tab = pl.BlockSpec((tm, LANES), lambda b, j: (j, 0))

    gains = [_tile_gain(q_g), _tile_gain(k_g), _tile_gain(mq_g)]
    if has_idx:
        gains.append(_tile_gain(idx_g))
    ng = norm_g.astype(F32)[None, :]
    inputs = [x, ng, w, cs, sn] + gains
    in_specs = [pl.BlockSpec((1, tm, d), lambda b, j: (b, j, 0)), const2(ng), const2(w), tab, tab]
    in_specs += [const2(g) for g in gains]

    outs = [row(Q_W, BF16), row(KV_W, F32), row(KV_W, F32), row(MQ_W, BF16), row(D_MIX, BF16)]
    if has_idx:
        outs += [row(IDX_W, BF16), row(LANES, F32)]
    out_specs = [o[0] for o in outs]
    out_shape = [o[1] for o in outs]
    if has_means:
        nb = tm // MOBA_BLOCK
        out_specs.append(pl.BlockSpec((1, 1, nb, KV_W), lambda b, j: (b, j, 0, 0)))
        out_shape.append(jax.ShapeDtypeStruct((bx, nt, nb, KV_W), F32))

    res = pl.pallas_call(
        functools.partial(_proj_body, has_idx=has_idx, has_means=has_means, tm=tm),
        grid=(bx, nt), in_specs=in_specs, out_specs=out_specs, out_shape=out_shape,
        compiler_params=_params("parallel", "parallel"), name="project",
    )(*inputs)
    res = list(res)
    out = dict(q=res[0], k=res[1], v=res[2], qm=res[3], gate=res[4])
    if has_idx:
        out.update(qi=res[5], kiw=res[6])
    if has_means:
        out["kmeans"] = res[-1].reshape(bx, tx // MOBA_BLOCK, KV_W)
    return out


def _memkv_body(mem_ref, g_ref, w_ref, gk_ref, mk_ref, mv_ref):
    x = mem_ref[0]
    ms = jnp.mean(x * x, axis=-1, keepdims=True)
    h = (x * lax.rsqrt(ms + RMS_EPS) * g_ref[...]).astype(BF16)
    hm = _head_group_matrix()
    y = _dot(h, w_ref[:, :MQ_W])
    for c in range(MQ_W // LANES):
        mk_ref[0, :, c * LANES:(c + 1) * LANES] = _head_rms(y[:, c * LANES:(c + 1) * LANES], gk_ref[...], hm)
    mv_ref[0] = _dot(h, w_ref[:, MQ_W:])


def _memory_kv(mem, mem_norm_g, w_mem_kv, mk_g):
    b, m, d = mem.shape
    g = mem_norm_g.astype(F32)[None, :]
    w = w_mem_kv.astype(BF16)
    gk = _tile_gain(mk_g)
    spec = pl.BlockSpec((1, m, MQ_W), lambda i: (i, 0, 0))
    shape = jax.ShapeDtypeStruct((b, m, MQ_W), F32)
    return pl.pallas_call(
        _memkv_body, grid=(b,),
        in_specs=[pl.BlockSpec((1, m, d), lambda i: (i, 0, 0)), pl.BlockSpec(g.shape, lambda i: (0, 0)),
                  pl.BlockSpec(w.shape, lambda i: (0, 0)), pl.BlockSpec(gk.shape, lambda i: (0, 0))],
        out_specs=[spec, spec], out_shape=[shape, shape],
        compiler_params=_params("parallel"), name="memory_kv",
    )(mem, g, w, gk)


def _finish_body(x_ref, mix_ref, qm_ref, gate_ref, mk_ref, mv_ref, w_ref, o_ref, *, mem_transposed):
    qm = qm_ref[0].astype(F32)
    mk = mk_ref[0]
    mv = mv_ref[0]
    mem = []
    for h in range(MEM_HEADS):
        sl = slice(h * HEAD_DIM, (h + 1) * HEAD_DIM)
        qh = qm[:, sl].astype(BF16)
        if mem_transposed:
            s = _dot(qh, mk[sl, :].astype(BF16)) * (HEAD_DIM ** -0.5)
        else:
            s = _dot_nt(qh, mk[:, sl].astype(BF16)) * (HEAD_DIM ** -0.5)
        e = jnp.exp(s - jnp.max(s, axis=1, keepdims=True))
        p = (e / jnp.sum(e, axis=1, keepdims=True)).astype(BF16)
        if mem_transposed:
            mem.append(_dot_nt(p, mv[sl, :].astype(BF16)))
        else:
            mem.append(_dot(p, mv[:, sl].astype(BF16)))
    cat = jnp.concatenate([mix_ref[0].astype(F32)] + mem, axis=1)
    g = gate_ref[0].astype(F32)
    act = cat * (g * jax.nn.sigmoid(g))
    o_ref[0] = x_ref[0] + _dot(act.astype(BF16), w_ref[...])


def _finish(x, mix, qm, gate, mk, mv, w_out, *, tm, mem_transposed):
    bx, tx, d = x.shape
    w = w_out.astype(BF16)
    row = lambda c: pl.BlockSpec((1, tm, c), lambda b, j: (b, j, 0))
    memspec = pl.BlockSpec((1,) + mk.shape[1:], lambda b, j: (b, 0, 0))
    return pl.pallas_call(
        functools.partial(_finish_body, mem_transposed=mem_transposed), grid=(bx, tx // tm),
        in_specs=[row(d), row(Q_W), row(MQ_W), row(D_MIX), memspec, memspec,
                  pl.BlockSpec(w.shape, lambda b, j: (0, 0))],
        out_specs=row(d), out_shape=jax.ShapeDtypeStruct((bx, tx, d), F32),
        compiler_params=_params("parallel", "parallel"), name="finish",
    )(x, mix, qm, gate, mk, mv, w)


def _count_chunks(kk, pred, c):
    for j in range(kk.shape[1] // LANES):
        c = c + jnp.where(pred(kk[:, j * LANES:(j + 1) * LANES]), 1.0, 0.0)
    return c


def _lane_total(c):
    return jnp.broadcast_to(jnp.sum(c, axis=1, keepdims=True), c.shape)


def _tile_counter(keys_ref, n_tiles, rows):
    def count(pred):
        c = lax.fori_loop(0, n_tiles, lambda t, c: _count_chunks(keys_ref[t], pred, c),
                          jnp.zeros((rows, LANES), F32))
        return _lane_total(c)
    return count


def _kth_largest(count, rows, n_sel):
    def step(i, t):
        cand = t ^ lax.shift_left(jnp.int32(1), 31 - i)
        cnt = count(lambda kk: kk >= cand)
        return jnp.where(cnt >= n_sel, cand, t)

    t = lax.fori_loop(0, 32, step, jnp.full((rows, LANES), INT32_MIN, jnp.int32))
    need = n_sel - count(lambda kk: kk > t)
    return t, need


def _selection_bias(key, t, need, carry, tri, allowed):
    tk = key.shape[1]
    reps = tk // LANES
    tb = jnp.tile(t, (1, reps))
    eq = key == tb
    eqf = jnp.where(eq, 1.0, 0.0)
    rank = _dot(eqf.astype(BF16), tri) + jnp.tile(carry, (1, reps))
    keep = (key > tb) | (eq & (rank <= jnp.tile(need, (1, reps))))
    if allowed is not None:
        keep = keep & allowed
    bias = jnp.where(keep, 0.0, NEG_BIG)
    carry = carry + jnp.broadcast_to(jnp.sum(eqf, axis=1, keepdims=True), carry.shape)
    return bias, carry


def _upper_tri(tk):
    return jnp.where(_iota((tk, tk), 0) <= _iota((tk, tk), 1), 1.0, 0.0).astype(BF16)


def _dsa_prompt_body(qi_ref, wq_ref, q_ref, kiw_ref, k_ref, v_ref, o_ref,
                     keys_ref, qs_ref, m_ref, l_ref, acc_ref, *, tq, tk, n_sel):
    i = pl.program_id(1)
    n_kt = (i * tq + tq + tk - 1) // tk
    row = i * tq + _iota((tq, tk), 0)
    col0 = _iota((tq, tk), 1)

    qi = qi_ref[0].astype(F32)
    qih = [qi[:, h * IDX_DIM:(h + 1) * IDX_DIM].astype(BF16) for h in range(IDX_HEADS)]
    wi = wq_ref[0][:, IDX_DIM:IDX_DIM + IDX_HEADS] * (IDX_DIM ** -0.5)
    wih = [wi[:, h:h + 1] for h in range(IDX_HEADS)]

    def score_tile(kt, _):
        koff = pl.multiple_of(kt * tk, tk)
        kib = kiw_ref[0, pl.ds(koff, tk), :][:, :IDX_DIM].astype(BF16)
        acc = jnp.zeros((tq, tk), F32)
        for h in range(IDX_HEADS):
            acc = acc + jnp.maximum(_dot_nt(qih[h], kib), 0.0) * wih[h]
        acc = jnp.where(koff + col0 <= row, acc, -jnp.inf)
        keys_ref[kt] = _sortable(acc)
        return 0

    lax.fori_loop(0, n_kt, score_tile, 0)
    t, need = _kth_largest(_tile_counter(keys_ref, n_kt, tq), tq, n_sel)

    q = q_ref[0].astype(F32) * (HEAD_DIM ** -0.5)
    for h in range(N_HEADS):
        g, r = divmod(h, GROUP)
        qs_ref[g, r * tq:(r + 1) * tq, :] = q[:, h * HEAD_DIM:(h + 1) * HEAD_DIM].astype(BF16)
    _flash_init(m_ref, l_ref, acc_ref)
    tri = _upper_tri(tk)

    def attend(kt, carry):
        koff = pl.multiple_of(kt * tk, tk)
        bias, carry = _selection_bias(keys_ref[kt], t, need, carry, tri, koff + col0 <= row)
        bias3 = jnp.concatenate([bias] * GROUP, axis=0)
        kb = k_ref[0, pl.ds(koff, tk), :]
        vb = v_ref[0, pl.ds(koff, tk), :]
        for g in range(N_KV_HEADS):
            sl = slice(g * HEAD_DIM, (g + 1) * HEAD_DIM)
            s = _dot_nt(qs_ref[g], kb[:, sl].astype(BF16)) + bias3
            _flash_update(s, vb[:, sl].astype(BF16), m_ref, l_ref, acc_ref, g)
        return carry

    lax.fori_loop(0, n_kt, attend, jnp.zeros((tq, LANES), F32))
    o_ref[0] = _flash_finish(l_ref, acc_ref, tq).astype(o_ref.dtype)


def _dsa_prompt(q, k, v, qi, kiw, *, tq, tk):
    b, s, _ = q.shape
    n_sel = min(DSA_TOPK, s // 4)
    assert s % tq == 0 and s % tk == 0
    rowq = lambda c: pl.BlockSpec((1, tq, c), lambda bb, i: (bb, i, 0))
    full = lambda c: pl.BlockSpec((1, s, c), lambda bb, i: (bb, 0, 0))
    rows = GROUP * tq
    return pl.pallas_call(
        functools.partial(_dsa_prompt_body, tq=tq, tk=tk, n_sel=n_sel),
        grid=(b, s // tq),
        in_specs=[rowq(IDX_W), rowq(LANES), rowq(Q_W), full(LANES), full(KV_W), full(KV_W)],
        out_specs=rowq(Q_W), out_shape=jax.ShapeDtypeStruct((b, s, Q_W), BF16),
        scratch_shapes=[pltpu.VMEM((s // tk, tq, tk), jnp.int32),
                        pltpu.VMEM((N_KV_HEADS, rows, HEAD_DIM), BF16),
                        pltpu.VMEM((N_KV_HEADS, rows, LANES), F32),
                        pltpu.VMEM((N_KV_HEADS, rows, LANES), F32),
                        pltpu.VMEM((N_KV_HEADS, rows, HEAD_DIM), F32)],
        compiler_params=_params("parallel", "arbitrary"), name="dsa_prompt",
    )(qi, kiw, q, kiw, k, v)


def _moba_block_bias(block_scores, n_valid):
    rows = block_scores.shape[0]
    lane_i = _iota((rows, LANES), 1)
    lane = lane_i.astype(F32)
    bs = jnp.where(lane_i < n_valid, block_scores, -jnp.inf)
    bias = jnp.full((rows, LANES), NEG_BIG, F32)
    for _ in range(MOBA_TOPK):
        mx = jnp.max(bs, axis=1, keepdims=True)
        idx = jnp.min(jnp.where(bs == mx, lane, float(LANES)), axis=1, keepdims=True)
        hit = lane == idx
        bias = jnp.where(hit & (mx > -jnp.inf), 0.0, bias)
        bs = jnp.where(hit, -jnp.inf, bs)
    return bias


def _moba_prompt_body(q_ref, km_ref, k_ref, v_ref, o_ref, qs_ref, m_ref, l_ref, acc_ref,
                      *, n_cand):
    tq = MOBA_BLOCK
    i = pl.program_id(1)
    q = q_ref[0].astype(F32)
    km = km_ref[0]
    lane = _iota((tq, LANES), 1)
    n_valid = jnp.minimum(i, n_cand)
    for h in range(N_HEADS):
        g, r = divmod(h, GROUP)
        qh = q[:, h * HEAD_DIM:(h + 1) * HEAD_DIM]
        kmg = km[:, g * HEAD_DIM:(g + 1) * HEAD_DIM].astype(BF16)
        bias = _moba_block_bias(_dot_nt(qh.astype(BF16), kmg), n_valid)
        bias = jnp.where(lane == i, 0.0, bias)
        aug = jnp.concatenate([qh * (HEAD_DIM ** -0.5), bias[:, :HEAD_DIM]], axis=1)
        qs_ref[g, r * tq:(r + 1) * tq, :] = aug.astype(BF16)
    _flash_init(m_ref, l_ref, acc_ref)
    lane_k = _iota((tq, HEAD_DIM), 1)

    def block(j, causal_bias):
        koff = pl.multiple_of(j * tq, tq)
        kb = k_ref[0, pl.ds(koff, tq), :]
        vb = v_ref[0, pl.ds(koff, tq), :]
        onehot = jnp.where(lane_k == j, 1.0, 0.0)
        for g in range(N_KV_HEADS):
            sl = slice(g * HEAD_DIM, (g + 1) * HEAD_DIM)
            kaug = jnp.concatenate([kb[:, sl], onehot], axis=1).astype(BF16)
            s = _dot_nt(qs_ref[g], kaug)
            if causal_bias is not None:
                s = s + causal_bias
            _flash_update(s, vb[:, sl].astype(BF16), m_ref, l_ref, acc_ref, g)

    def past(j, _):
        block(j, None)
        return 0

    lax.fori_loop(0, i, past, 0)
    tri_mask = _iota((tq, tq), 1) <= _iota((tq, tq), 0)
    cb = jnp.where(tri_mask, 0.0, NEG_BIG)
    block(i, jnp.concatenate([cb] * GROUP, axis=0))
    o_ref[0] = _flash_finish(l_ref, acc_ref, tq).astype(o_ref.dtype)


def _moba_prompt(q, k, v, kmeans):
    b, s, _ = q.shape
    tq = MOBA_BLOCK
    assert s % tq == 0 and s // tq <= HEAD_DIM
    n_cand = (s - 1) // MOBA_BLOCK
    assert min(MOBA_TOPK, n_cand) == MOBA_TOPK
    kmp = jnp.pad(kmeans, ((0, 0), (0, LANES - kmeans.shape[1]), (0, 0)))
    rowq = lambda c: pl.BlockSpec((1, tq, c), lambda bb, i: (bb, i, 0))
    full = lambda c: pl.BlockSpec((1, s, c), lambda bb, i: (bb, 0, 0))
    rows = GROUP * tq
    return pl.pallas_call(
        functools.partial(_moba_prompt_body, n_cand=n_cand),
        grid=(b, s // tq),
        in_specs=[rowq(Q_W), pl.BlockSpec((1, LANES, KV_W), lambda bb, i: (bb, 0, 0)),
                  full(KV_W), full(KV_W)],
        out_specs=rowq(Q_W), out_shape=jax.ShapeDtypeStruct((b, s, Q_W), BF16),
        scratch_shapes=[pltpu.VMEM((N_KV_HEADS, rows, LANES), BF16),
                        pltpu.VMEM((N_KV_HEADS, rows, LANES), F32),
                        pltpu.VMEM((N_KV_HEADS, rows, LANES), F32),
                        pltpu.VMEM((N_KV_HEADS, rows, HEAD_DIM), F32)],
        compiler_params=_params("parallel", "arbitrary"), name="moba_prompt",
    )(q, kmp, k, v)


ROWS8 = SUBLANES
STACK = N_HEADS * ROWS8


def _stack_queries(q, qs_ref):
    q = q * (HEAD_DIM ** -0.5)
    for h in range(N_HEADS):
        g = h // GROUP
        parts = []
        if g > 0:
            parts.append(jnp.zeros((ROWS8, g * HEAD_DIM), F32))
        parts.append(q[:, h * HEAD_DIM:(h + 1) * HEAD_DIM])
        if g < N_KV_HEADS - 1:
            parts.append(jnp.zeros((ROWS8, (N_KV_HEADS - 1 - g) * HEAD_DIM), F32))
        qs_ref[h * ROWS8:(h + 1) * ROWS8, :] = jnp.concatenate(parts, axis=1).astype(BF16)


def _unstack_output(l_ref, acc_ref):
    outs = []
    for h in range(N_HEADS):
        g = h // GROUP
        rs = slice(h * ROWS8, (h + 1) * ROWS8)
        outs.append(acc_ref[0, rs, g * HEAD_DIM:(g + 1) * HEAD_DIM] / l_ref[0, rs, :HEAD_DIM])
    return jnp.concatenate(outs, axis=1)


def _pad_rows(a, rows):
    return jnp.concatenate([a, jnp.zeros((rows - a.shape[0], a.shape[1]), a.dtype)], axis=0)


def _new_key_mask(rows, t_len):
    t = _iota((rows, LANES), 0) % ROWS8
    lane = _iota((rows, LANES), 1)
    return (lane <= t) & (lane < t_len)


def _pages(refs):
    return jnp.concatenate([r[0, 0] for r in refs], axis=1).astype(BF16)


def _dsa_sample_body(pt_ref, qi_ref, kiw_ref, q_ref, kn_ref, vn_ref, *rest, pc, n_chunks, t_len, n_sel):
    ki_refs, k_refs, v_refs = rest[:pc], rest[pc:2 * pc], rest[2 * pc:3 * pc]
    o_ref = rest[3 * pc]
    keys_ref, keysn_ref, thr_ref, need_ref, carry_ref, qs_ref, m_ref, l_ref, acc_ref = rest[3 * pc + 1:]
    ph = pl.program_id(1)
    c = pl.program_id(2)
    last = n_chunks - 1

    def scores(ki, transposed):
        qi = qi_ref[0].astype(F32)
        wi = kiw_ref[0][:, IDX_DIM:IDX_DIM + IDX_HEADS] * (IDX_DIM ** -0.5)
        acc = None
        for h in range(IDX_HEADS):
            qh = qi[:, h * IDX_DIM:(h + 1) * IDX_DIM].astype(BF16)
            s = _dot(qh, ki) if transposed else _dot_nt(qh, ki)
            term = jnp.maximum(s, 0.0) * wi[:, h:h + 1]
            acc = term if acc is None else acc + term
        return acc

    @pl.when(ph == 0)
    def _select():
        keys_ref[c] = _sortable(scores(_pages(ki_refs), True))

        @pl.when(c == last)
        def _threshold():
            kin = _pad_rows(kiw_ref[0][:, :IDX_DIM], LANES).astype(BF16)
            sn = jnp.where(_new_key_mask(ROWS8, t_len), scores(kin, False), -jnp.inf)
            keysn_ref[...] = _sortable(sn)

            def count(pred):
                cc = lax.fori_loop(0, n_chunks, lambda t, cc: _count_chunks(keys_ref[t], pred, cc),
                                   jnp.zeros((ROWS8, LANES), F32))
                return _lane_total(_count_chunks(keysn_ref[...], pred, cc))

            t, need = _kth_largest(count, ROWS8, n_sel)
            thr_ref[...] = t
            need_ref[...] = need
            carry_ref[...] = jnp.zeros(carry_ref.shape, F32)
            _stack_queries(q_ref[0].astype(F32), qs_ref)
            _flash_init(m_ref, l_ref, acc_ref)

    @pl.when(ph == 1)
    def _attend():
        t = thr_ref[...]
        need = need_ref[...]
        keys = keys_ref[c]
        carry = carry_ref[...]
        sub = 2 * LANES
        tri = _upper_tri(sub)
        biases = []
        for j in range(keys.shape[1] // sub):
            b_, carry = _selection_bias(keys[:, j * sub:(j + 1) * sub], t, need, carry, tri, None)
            biases.append(b_)
        carry_ref[...] = carry
        bias = jnp.concatenate(biases, axis=1)
        s = _dot(qs_ref[...], _pages(k_refs)) + jnp.concatenate([bias] * N_HEADS, axis=0)
        _flash_update(s, _pages(v_refs), m_ref, l_ref, acc_ref, 0, v_transposed=True)

        @pl.when(c == last)
        def _new_rows():
            bn, _ = _selection_bias(keysn_ref[...], t, need, carry, _upper_tri(LANES),
                                    _new_key_mask(ROWS8, t_len))
            kn = _pad_rows(kn_ref[0], LANES).astype(BF16)
            vn = _pad_rows(vn_ref[0], LANES).astype(BF16)
            sn = _dot_nt(qs_ref[...], kn) + jnp.concatenate([bn] * N_HEADS, axis=0)
            _flash_update(sn, vn, m_ref, l_ref, acc_ref, 0)
            o_ref[0] = _unstack_output(l_ref, acc_ref).astype(o_ref.dtype)


def _page_spec(width, layer, page_of):
    return lambda psz: pl.BlockSpec(
        (1, 1, width, psz), lambda b, ph, c, pt: (layer, pt[b, page_of(ph, c)], 0, 0))


def _sample_specs(pc, n_chunks, psz, layer_k, layer_i, with_idx, k_every_phase):
    last = n_chunks - 1
    specs = []
    if with_idx:
        for r in range(pc):
            specs.append(_page_spec(IDX_DIM, layer_i, lambda ph, c, r=r: ((1 - ph) * c + ph * last) * pc + r)(psz))
    for r in range(pc):
        if k_every_phase:
            specs.append(_page_spec(KV_W, layer_k, lambda ph, c, r=r: c * pc + r)(psz))
        else:
            specs.append(_page_spec(KV_W, layer_k, lambda ph, c, r=r: ph * c * pc + r)(psz))
    for r in range(pc):
        specs.append(_page_spec(KV_W, layer_k, lambda ph, c, r=r: ph * c * pc + r)(psz))
    return specs


def _sample_scratch():
    return [pltpu.VMEM((STACK, KV_W), BF16), pltpu.VMEM((1, STACK, LANES), F32),
            pltpu.VMEM((1, STACK, LANES), F32), pltpu.VMEM((1, STACK, KV_W), F32)]


def _dsa_sample(q, k_new, v_new, qi, kiw, pool_k, pool_v, pool_ki, layer_k, layer_i, page_table,
                *, t_len, pc):
    db = q.shape[0]
    n_pages = page_table.shape[1]
    psz = pool_k.shape[3]
    assert n_pages % pc == 0 and psz == LANES
    n_chunks = n_pages // pc
    n_sel = min(DSA_TOPK, (n_pages * psz + t_len) // 4)
    rowb = lambda c_: pl.BlockSpec((1, ROWS8, c_), lambda b, ph, c, pt: (b, 0, 0))
    in_specs = [rowb(IDX_W), rowb(LANES), rowb(Q_W), rowb(KV_W), rowb(KV_W)]
    in_specs += _sample_specs(pc, n_chunks, psz, layer_k, layer_i, True, False)
    grid_spec = pltpu.PrefetchScalarGridSpec(
        num_scalar_prefetch=1, grid=(db, 2, n_chunks), in_specs=in_specs, out_specs=rowb(Q_W),
        scratch_shapes=[pltpu.VMEM((n_chunks, ROWS8, pc * psz), jnp.int32),
                        pltpu.VMEM((ROWS8, LANES), jnp.int32),
                        pltpu.VMEM((ROWS8, LANES), jnp.int32),
                        pltpu.VMEM((ROWS8, LANES), F32),
                        pltpu.VMEM((ROWS8, LANES), F32)] + _sample_scratch())
    return pl.pallas_call(
        functools.partial(_dsa_sample_body, pc=pc, n_chunks=n_chunks, t_len=t_len, n_sel=n_sel),
        grid_spec=grid_spec, out_shape=jax.ShapeDtypeStruct((db, ROWS8, Q_W), BF16),
        compiler_params=_params("parallel", "arbitrary", "arbitrary"), name="dsa_sample",
    )(page_table, qi, kiw, q, k_new, v_new, *([pool_ki] * pc), *([pool_k] * pc), *([pool_v] * pc))


def _moba_sample_body(pt_ref, q_ref, kn_ref, vn_ref, *rest, pc, n_chunks, t_len, n_blocks, ppb):
    k_refs, v_refs = rest[:pc], rest[pc:2 * pc]
    o_ref = rest[2 * pc]
    means_ref, bias_ref, qs_ref, m_ref, l_ref, acc_ref = rest[2 * pc + 1:]
    ph = pl.program_id(1)
    c = pl.program_id(2)
    last = n_chunks - 1
    bpc = pc // ppb

    @pl.when(ph == 0)
    def _select():
        @pl.when(c == 0)
        def _():
            means_ref[...] = jnp.zeros(means_ref.shape, F32)

        lane_m = _iota(means_ref.shape, 1)
        means = means_ref[...]
        for j in range(bpc):
            blk = jnp.concatenate([r[0, 0] for r in k_refs[j * ppb:(j + 1) * ppb]], axis=1)
            means = jnp.where(lane_m == c * bpc + j, jnp.mean(blk, axis=1, keepdims=True), means)
        means_ref[...] = means

        @pl.when(c == last)
        def _():
            q = q_ref[0].astype(F32)
            for h in range(N_HEADS):
                g = h // GROUP
                kmg = means[g * HEAD_DIM:(g + 1) * HEAD_DIM, :].astype(BF16)
                qh = q[:, h * HEAD_DIM:(h + 1) * HEAD_DIM].astype(BF16)
                bias_ref[h * ROWS8:(h + 1) * ROWS8, :] = _moba_block_bias(_dot(qh, kmg), n_blocks)
            _stack_queries(q, qs_ref)
            _flash_init(m_ref, l_ref, acc_ref)

    @pl.when(ph == 1)
    def _attend():
        bias_all = bias_ref[...]
        lane = _iota((STACK, LANES), 1)
        cols = []
        for j in range(bpc):
            col = jnp.sum(jnp.where(lane == c * bpc + j, bias_all, 0.0), axis=1, keepdims=True)
            cols.append(jnp.broadcast_to(col, (STACK, MOBA_BLOCK)))
        s = _dot(qs_ref[...], _pages(k_refs)) + jnp.concatenate(cols, axis=1)
        _flash_update(s, _pages(v_refs), m_ref, l_ref, acc_ref, 0, v_transposed=True)

        @pl.when(c == last)
        def _own_block():
            kn = _pad_rows(kn_ref[0], LANES).astype(BF16)
            vn = _pad_rows(vn_ref[0], LANES).astype(BF16)
            sn = _dot_nt(qs_ref[...], kn) + jnp.where(_new_key_mask(STACK, t_len), 0.0, NEG_BIG)
            _flash_update(sn, vn, m_ref, l_ref, acc_ref, 0)
            o_ref[0] = _unstack_output(l_ref, acc_ref).astype(o_ref.dtype)


def _moba_sample(q, k_new, v_new, pool_k, pool_v, layer_k, page_table, *, t_len, pc):
    db = q.shape[0]
    n_pages = page_table.shape[1]
    psz = pool_k.shape[3]
    past = n_pages * psz
    ppb = MOBA_BLOCK // psz
    assert psz == LANES and MOBA_BLOCK % psz == 0 and past % MOBA_BLOCK == 0 and t_len <= ROWS8
    assert n_pages % pc == 0 and pc % ppb == 0
    n_chunks = n_pages // pc
    n_blocks = (past + t_len - 1) // MOBA_BLOCK
    assert n_blocks == past // MOBA_BLOCK and MOBA_TOPK <= n_blocks <= LANES
    rowb = lambda c_: pl.BlockSpec((1, ROWS8, c_), lambda b, ph, c, pt: (b, 0, 0))
    in_specs = [rowb(Q_W), rowb(KV_W), rowb(KV_W)]
    in_specs += _sample_specs(pc, n_chunks, psz, layer_k, 0, False, True)
    grid_spec = pltpu.PrefetchScalarGridSpec(
        num_scalar_prefetch=1, grid=(db, 2, n_chunks), in_specs=in_specs, out_specs=rowb(Q_W),
        scratch_shapes=[pltpu.VMEM((KV_W, LANES), F32), pltpu.VMEM((STACK, LANES), F32)] + _sample_scratch())
    return pl.pallas_call(
        functools.partial(_moba_sample_body, pc=pc, n_chunks=n_chunks, t_len=t_len,
                          n_blocks=n_blocks, ppb=ppb),
        grid_spec=grid_spec, out_shape=jax.ShapeDtypeStruct((db, ROWS8, Q_W), BF16),
        compiler_params=_params("parallel", "arbitrary", "arbitrary"), name="moba_sample",
    )(page_table, q, k_new, v_new, *([pool_k] * pc), *([pool_v] * pc))


PROMPT_TM = 512
DSA_TQ = 256
DSA_TK = 256
PAGES_PER_CHUNK = 16


def kernel(x_prompt, x_sample, cache_k, cache_v, cache_kidx, cache_mem_k, cache_mem_v, page_table,
           mem_prompt, norm_g, w_in_a, w_in_b, q_norm_g, k_norm_g, idx_k_norm_g, mem_norm_g,
           w_mem_kv, mem_q_norm_g, mem_k_norm_g, w_out):
    b, s, d = x_prompt.shape
    db, t_len, _ = x_sample.shape
    depth = norm_g.shape[0]
    n_pool, psz = cache_k.shape[1], cache_k.shape[2]
    past = page_table.shape[1] * psz
    mem_len = mem_prompt.shape[1]
    assert t_len <= ROWS8

    pool_k = jnp.transpose(cache_k, (0, 1, 3, 4, 2)).reshape(depth, n_pool, KV_W, psz)
    pool_v = jnp.transpose(cache_v, (0, 1, 3, 4, 2)).reshape(depth, n_pool, KV_W, psz)
    pool_ki = jnp.transpose(cache_kidx, (0, 1, 3, 2))
    mem_k_s = jnp.transpose(cache_mem_k, (0, 1, 3, 4, 2)).reshape(depth, db, MQ_W, mem_len)
    mem_v_s = jnp.transpose(cache_mem_v, (0, 1, 3, 4, 2)).reshape(depth, db, MQ_W, mem_len)
    pos_p = jnp.arange(s, dtype=jnp.int32)
    pos_s = past + jnp.arange(db * ROWS8, dtype=jnp.int32) % ROWS8
    xp = x_prompt
    xs = jnp.pad(x_sample, ((0, 0), (0, ROWS8 - t_len), (0, 0)))
    unflat = lambda a: a.reshape(db, ROWS8, a.shape[-1])

    k_p, v_p, ki_p, mk_p, mv_p, k_s, v_s, ki_s = ([] for _ in range(8))
    for i in range(depth):
        j = i // 2
        use_dsa = i % 2 == 0
        w_in = w_in_a[j] if use_dsa else w_in_b[j]
        idx_g = idx_k_norm_g[j] if use_dsa else None
        gains = (norm_g[i], w_in, q_norm_g[i], k_norm_g[i], mem_q_norm_g[i], idx_g)
        pp = _project(xp, pos_p, *gains, tm=PROMPT_TM, has_means=not use_dsa)
        ps = _project(xs.reshape(1, db * ROWS8, d), pos_s, *gains, tm=db * ROWS8, has_means=False)
        ps = {name: unflat(a[0]) for name, a in ps.items()}
        if use_dsa:
            mix_p = _dsa_prompt(pp["q"], pp["k"], pp["v"], pp["qi"], pp["kiw"], tq=DSA_TQ, tk=DSA_TK)
            mix_s = _dsa_sample(ps["q"], ps["k"], ps["v"], ps["qi"], ps["kiw"], pool_k, pool_v, pool_ki,
                                i, j, page_table, t_len=t_len, pc=PAGES_PER_CHUNK)
            ki_p.append(pp["kiw"][..., :IDX_DIM])
            ki_s.append(ps["kiw"][:, :t_len, :IDX_DIM])
        else:
            mix_p = _moba_prompt(pp["q"], pp["k"], pp["v"], pp["kmeans"])
            mix_s = _moba_sample(ps["q"], ps["k"], ps["v"], pool_k, pool_v, i, page_table,
                                 t_len=t_len, pc=PAGES_PER_CHUNK)
        mk, mv = _memory_kv(mem_prompt, mem_norm_g[i], w_mem_kv[i], mem_k_norm_g[i])
        xp = _finish(xp, mix_p, pp["qm"], pp["gate"], mk, mv, w_out[i], tm=PROMPT_TM, mem_transposed=False)
        xs = _finish(xs, mix_s, ps["qm"], ps["gate"], mem_k_s[i], mem_v_s[i], w_out[i], tm=ROWS8,
                     mem_transposed=True)
        k_p.append(pp["k"].reshape(b, s, N_KV_HEADS, HEAD_DIM))
        v_p.append(pp["v"].reshape(b, s, N_KV_HEADS, HEAD_DIM))
        mk_p.append(mk.reshape(b, mem_len, MEM_HEADS, HEAD_DIM))
        mv_p.append(mv.reshape(b, mem_len, MEM_HEADS, HEAD_DIM))
        k_s.append(ps["k"][:, :t_len].reshape(db, t_len, N_KV_HEADS, HEAD_DIM))
        v_s.append(ps["v"][:, :t_len].reshape(db, t_len, N_KV_HEADS, HEAD_DIM))
    return (xp, xs[:, :t_len], jnp.stack(k_p), jnp.stack(v_p), jnp.stack(ki_p), jnp.stack(mk_p),
            jnp.stack(mv_p), jnp.stack(k_s), jnp.stack(v_s), jnp.stack(ki_s))
```
